```python
import math
import jax, jax.numpy as jnp
from jax import lax
import numpy as np

D_MODEL = 1024
BATCH = 8
SEQ = 2048
DEPTH = 2
DEC_BATCH = 128
DEC_SEQ = 8
PAST_LEN = 16384
PAGE_SIZE = 128

D_MIX = D_MODEL
N_MIXERS = 4
GROUP_W = D_MIX // N_MIXERS
HEAD_DIM = 64
N_HEADS = GROUP_W // HEAD_DIM
D_FF = ((8 * D_MODEL // 3 + 127) // 128) * 128
CONV_W = 4
CHUNK = 64
ROPE_BASE = 10000.0
EPS = 1e-6
NEG_BIG = -1e30
LB_FLOOR = 1e-30
SPLIT_SIZES = (3 * GROUP_W, GROUP_W, N_HEADS, N_HEADS,
               GROUP_W, GROUP_W, GROUP_W, GROUP_W,
               GROUP_W, GROUP_W, GROUP_W, GROUP_W, N_HEADS, N_HEADS,
               GROUP_W, GROUP_W, GROUP_W, GROUP_W)
N_IN = 16 * GROUP_W + 4 * N_HEADS

kernel_name = 'hybrid_parallel_recurrent_heads_decode_step'

F32 = jnp.float32


def rmsnorm(x, g):
    x32 = x.astype(F32)
    y = x32 * lax.rsqrt(jnp.mean(x32 * x32, axis=-1, keepdims=True) + EPS)
    return (y * g.astype(F32)).astype(x.dtype)


def head_rmsnorm(o, g):
    y = o * lax.rsqrt(jnp.mean(o * o, axis=-1, keepdims=True) + EPS)
    return y * g.astype(F32).reshape(N_HEADS, HEAD_DIM)


def head_groupnorm(o, g):
    c = o - jnp.mean(o, axis=-1, keepdims=True)
    y = c * lax.rsqrt(jnp.mean(c * c, axis=-1, keepdims=True) + EPS)
    return y * g.astype(F32).reshape(N_HEADS, HEAD_DIM)


def l2norm(x):
    return x * lax.rsqrt(jnp.sum(x * x, axis=-1, keepdims=True) + EPS)


def heads(t):
    b, l, _ = t.shape
    return t.reshape(b, l, N_HEADS, HEAD_DIM)


def masked_exp(mask, logw):
    return jnp.where(mask, jnp.exp(jnp.where(mask, logw, 0.0)), 0.0)


def swiglu(x, wg, wu, wd):
    a = jnp.einsum('bld,df->blf', x, wg)
    u = jnp.einsum('bld,df->blf', x, wu)
    return jnp.einsum('blf,fd->bld', jax.nn.silu(a) * u, wd)


def rope(x, pos):
    half = HEAD_DIM // 2
    inv = ROPE_BASE ** (-jnp.arange(half, dtype=F32) / half)
    ang = pos[:, None] * inv[None, :]
    cos = jnp.cos(ang)[None, :, None, :]
    sin = jnp.sin(ang)[None, :, None, :]
    x1, x2 = x[..., :half], x[..., half:]
    return jnp.concatenate([x1 * cos - x2 * sin, x2 * cos + x1 * sin], axis=-1)


def chunk_len(L):
    return math.gcd(L, CHUNK)


def to_chunks(x, c):
    b, l, h, d = x.shape
    return x.reshape(b, l // c, c, h, d).transpose(1, 0, 3, 2, 4)


def to_chunks_s(x, c):
    b, l, h = x.shape
    return x.reshape(b, l // c, c, h).transpose(1, 0, 3, 2)


def from_chunks(x):
    n, b, h, c, d = x.shape
    return x.transpose(1, 0, 3, 2, 4).reshape(b, n * c, h, d)


def gated_delta_chunked(q, k, v, log_alpha, beta, S0):
    B, L, H, _ = q.shape
    C = chunk_len(L)
    causal = jnp.tril(jnp.ones((C, C), bool))
    strict = jnp.tril(jnp.ones((C, C), bool), -1)

    def step(S, inp):
        qi, ki, vi, la, bt = inp
        g = jnp.cumsum(la, axis=-1)
        decay = masked_exp(causal, g[..., :, None] - g[..., None, :])
        a_mat = jnp.where(strict, bt[..., :, None] * decay * jnp.einsum('bhtd,bhsd->bhts', ki, ki), 0.0)
        rhs = jnp.concatenate([bt[..., None] * vi, (bt * jnp.exp(g))[..., None] * ki], axis=-1)
        sol = lax.linalg.triangular_solve(a_mat, rhs, left_side=True, lower=True, unit_diagonal=True)
        dv = vi.shape[-1]
        u = sol[..., :dv] - jnp.einsum('bhtk,bhkv->bhtv', sol[..., dv:], S)
        att = jnp.einsum('bhtd,bhsd->bhts', qi, ki) * decay
        o = jnp.exp(g)[..., None] * jnp.einsum('bhtk,bhkv->bhtv', qi, S) + jnp.einsum('bhts,bhsv->bhtv', att, u)
        gL = g[..., -1:]
        S_new = jnp.exp(gL)[..., None] * S + jnp.einsum('bhsk,bhsv->bhkv', ki * jnp.exp(gL - g)[..., None], u)
        return S_new, o

    S, o = lax.scan(step, S0, (to_chunks(q, C), to_chunks(k, C), to_chunks(v, C),
                               to_chunks_s(log_alpha, C), to_chunks_s(beta, C)))
    return from_chunks(o), S


def retention_chunked(q, k, v, log_gamma, S0):
    B, L, H, _ = q.shape
    C = chunk_len(L)
    idx = jnp.arange(C, dtype=F32)
    causal = jnp.tril(jnp.ones((C, C), bool))[None]
    decay = masked_exp(causal, (idx[:, None] - idx[None, :])[None] * log_gamma[:, None, None])
    q_dec = jnp.exp((idx + 1.0) * log_gamma[:, None])
    k_dec = jnp.exp((C - 1.0 - idx) * log_gamma[:, None])
    chunk_dec = jnp.exp(C * log_gamma)

    def step(S, inp):
        qi, ki, vi = inp
        att = jnp.einsum('bhtd,bhsd->bhts', qi, ki) * decay
        o = jnp.einsum('bhts,bhsv->bhtv', att, vi) + q_dec[..., None] * jnp.einsum('bhtk,bhkv->bhtv', qi, S)
        S_new = chunk_dec[:, None, None] * S + jnp.einsum('bhsk,bhsv->bhkv', ki * k_dec[..., None], vi)
        return S_new, o

    S, o = lax.scan(step, S0, (to_chunks(q, C), to_chunks(k, C), to_chunks(v, C)))
    return from_chunks(o), S


def mlstm_chunked(q, k, v, i_pre, log_f, C0, n0, m0):
    B, L, H, _ = q.shape
    C = chunk_len(L)
    causal = jnp.tril(jnp.ones((C, C), bool))

    def step(carry, inp):
        Cs, ns, ms = carry
        qi, ki, vi, ii, lf = inp
        b = jnp.cumsum(lf, axis=-1)
        log_d = jnp.where(causal, b[..., :, None] - b[..., None, :] + ii[..., None, :], NEG_BIG)
        log_inter = b + ms[..., None]
        m_t = jnp.maximum(log_inter, jnp.max(log_d, axis=-1))
        w = masked_exp(causal, log_d - m_t[..., None])
        g_in = jnp.exp(log_inter - m_t)
        s = jnp.einsum('bhtd,bhsd->bhts', qi, ki) * w
        num = g_in[..., None] * jnp.einsum('bhtk,bhkv->bhtv', qi, Cs) + jnp.einsum('bhts,bhsv->bhtv', s, vi)
        den = g_in * jnp.einsum('bhtk,bhk->bht', qi, ns) + jnp.sum(s, axis=-1)
        h = num / jnp.maximum(jnp.abs(den), jnp.exp(-m_t))[..., None]
        m_new = m_t[..., -1]
        dec = jnp.exp(b[..., -1] + ms - m_new)
        w_s = jnp.exp(b[..., -1:] - b + ii - m_new[..., None])
        kw = ki * w_s[..., None]
        C_new = dec[..., None, None] * Cs + jnp.einsum('bhsk,bhsv->bhkv', kw, vi)
        n_new = dec[..., None] * ns + jnp.sum(kw, axis=-2)
        return (C_new, n_new, m_new), h

    (Cf, nf, mf), h = lax.scan(step, (C0, n0, m0),
                               (to_chunks(q, C), to_chunks(k, C), to_chunks(v, C),
                                to_chunks_s(i_pre, C), to_chunks_s(log_f, C)))
    return from_chunks(h), Cf, nf, mf


def hgrn2_chunked(q, k, v, log_f, S0):
    B, L, H, _ = q.shape
    C = chunk_len(L)
    causal = jnp.tril(jnp.ones((C, C), bool))[:, :, None]

    def step(S, inp):
        qi, ki, vi, lf = inp
        b = jnp.cumsum(lf, axis=-2)
        rel = masked_exp(causal, b[..., :, None, :] - b[..., None, :, :])
        att = jnp.einsum('bhtk,bhsk,bhtsk->bhts', qi, ki, rel)
        o = jnp.einsum('bhtk,bhkv->bhtv', qi * jnp.exp(b), S) + jnp.einsum('bhts,bhsv->bhtv', att, vi)
        bL = b[..., -1:, :]
        S_new = jnp.exp(bL[..., 0, :])[..., None] * S + jnp.einsum('bhsk,bhsv->bhkv', ki * jnp.exp(bL - b), vi)
        return S_new, o

    S, o = lax.scan(step, S0, (to_chunks(q, C), to_chunks(k, C), to_chunks(v, C), to_chunks(log_f, C)))
    return from_chunks(o), S


def token_mixers(h, pos, l, st, prm):
    dS, dconv, rS, mC, mn, mm, hS = st
    B, L, _ = h.shape
    z = jnp.einsum('bld,de->ble', h, prm['w_in'][l]).astype(F32)
    points = np.cumsum(SPLIT_SIZES)[:-1].tolist()
    (a_qkv, a_g, a_a, a_b, b_q, b_k, b_v, b_g,
     c_q, c_k, c_v, c_o, c_i, c_f, d_q, d_f, d_i, d_g) = jnp.split(z, points, axis=-1)

    xp = jnp.concatenate([dconv.astype(F32), a_qkv], axis=1)
    cw = prm['delta_conv_w'][l].astype(F32)
    conv = xp[:, 0:L] * cw[0]
    for j in range(1, CONV_W):
        conv = conv + xp[:, j:j + L] * cw[j]
    new_dconv = xp[:, L:]
    aq, ak, av = jnp.split(jax.nn.silu(conv), 3, axis=-1)
    aq = l2norm(heads(aq)) * HEAD_DIM ** -0.5
    ak = l2norm(heads(ak))
    log_alpha = -jnp.exp(prm['delta_a_log'][l].astype(F32)) * jax.nn.softplus(a_a + prm['delta_dt_bias'][l].astype(F32))
    beta = jax.nn.sigmoid(a_b)
    ao, new_dS = gated_delta_chunked(aq, ak, heads(av), log_alpha, beta, dS.astype(F32))
    ao = head_rmsnorm(ao, prm['delta_norm_g'][l]) * jax.nn.silu(heads(a_g))

    bq = rope(heads(b_q), pos)
    bk = rope(heads(b_k), pos) * HEAD_DIM ** -0.5
    log_gamma = jnp.log1p(-jnp.exp2(-5.0 - jnp.arange(N_HEADS, dtype=F32)))
    bo, new_rS = retention_chunked(bq, bk, heads(b_v), log_gamma, rS.astype(F32))
    bo = head_groupnorm(bo, prm['ret_norm_g'][l]) * jax.nn.silu(heads(b_g))

    i_pre = c_i + prm['mlstm_i_bias'][l].astype(F32)
    log_f = jax.nn.log_sigmoid(c_f + prm['mlstm_f_bias'][l].astype(F32))
    co, new_mC, new_mn, new_mm = mlstm_chunked(heads(c_q), heads(c_k) * HEAD_DIM ** -0.5, heads(c_v),
                                               i_pre, log_f, mC.astype(F32), mn.astype(F32), mm.astype(F32))
    co = head_rmsnorm(co, prm['mlstm_norm_g'][l]) * jax.nn.sigmoid(heads(c_o))

    sm = jax.nn.softmax(prm['hgrn_lb_logits'].astype(F32), axis=0)
    lb = (jnp.cumsum(sm, axis=0) - sm[0])[l].reshape(N_HEADS, HEAD_DIM)
    dfh = heads(d_f)
    log_f_d = jnp.logaddexp(jnp.log(jnp.maximum(lb, LB_FLOOR)), jnp.log1p(-lb) + jax.nn.log_sigmoid(dfh))
    k_d = (1.0 - lb) * jax.nn.sigmoid(-dfh)
    do, new_hS = hgrn2_chunked(jax.nn.silu(heads(d_q)), k_d, heads(d_i), log_f_d, hS.astype(F32))
    do = head_rmsnorm(do, prm['hgrn_norm_g'][l]) * jax.nn.silu(heads(d_g))

    o_cat = jnp.concatenate([ao, bo, co, do], axis=2).reshape(B, L, D_MIX)
    out = jnp.einsum('ble,ed->bld', o_cat.astype(h.dtype), prm['w_out'][l])
    return out, (new_dS, new_dconv, new_rS, new_mC, new_mn, new_mm, new_hS)


def decoder_layer(x, pos, l, st, prm):
    f1 = swiglu(rmsnorm(x, prm['ffn1_pre_g'][l]), prm['ffn1_w_gate'][l], prm['ffn1_w_up'][l], prm['ffn1_w_down'][l])
    x = x + 0.5 * rmsnorm(f1, prm['ffn1_post_g'][l])
    mix, new_st = token_mixers(rmsnorm(x, prm['mix_pre_g'][l]), pos, l, st, prm)
    x = x + rmsnorm(mix, prm['mix_post_g'][l])
    f2 = swiglu(rmsnorm(x, prm['ffn2_pre_g'][l]), prm['ffn2_w_gate'][l], prm['ffn2_w_up'][l], prm['ffn2_w_down'][l])
    x = x + 0.5 * rmsnorm(f2, prm['ffn2_post_g'][l])
    return x, new_st


def zero_states(b):
    return (jnp.zeros((DEPTH, b, N_HEADS, HEAD_DIM, HEAD_DIM), F32),
            jnp.zeros((DEPTH, b, CONV_W - 1, 3 * GROUP_W), F32),
            jnp.zeros((DEPTH, b, N_HEADS, HEAD_DIM, HEAD_DIM), F32),
            jnp.zeros((DEPTH, b, N_HEADS, HEAD_DIM, HEAD_DIM), F32),
            jnp.zeros((DEPTH, b, N_HEADS, HEAD_DIM), F32),
            jnp.zeros((DEPTH, b, N_HEADS), F32),
            jnp.zeros((DEPTH, b, N_HEADS, HEAD_DIM, HEAD_DIM), F32))


def run_group(x, pos0, states, prm):
    pos = pos0 + jnp.arange(x.shape[1], dtype=F32)
    new = []
    for l in range(DEPTH):
        st = tuple(s[l] for s in states)
        x, ns = decoder_layer(x, pos, l, st, prm)
        new.append(ns)
    stacked = tuple(jnp.stack([n[i] for n in new], axis=0) for i in range(len(states)))
    return x, stacked


def setup_inputs(seed: int = 0) -> dict:
    key = jax.random.key(seed)
    ks = iter(jax.random.split(key, 64))

    def nrm(shape, scale):
        return jax.random.normal(next(ks), shape, F32) * scale

    def gain(shape):
        return 1.0 + nrm(shape, 0.02)

    H, HD, G = N_HEADS, HEAD_DIM, GROUP_W
    d = {}
    d['x_prompt'] = nrm((BATCH, SEQ, D_MODEL), 1.0)
    d['x_sample'] = nrm((DEC_BATCH, DEC_SEQ, D_MODEL), 1.0)
    d['state_delta_S'] = nrm((DEPTH, DEC_BATCH, H, HD, HD), HD ** -0.5)
    d['state_delta_conv'] = nrm((DEPTH, DEC_BATCH, CONV_W - 1, 3 * G), 1.0)
    d['state_ret_S'] = nrm((DEPTH, DEC_BATCH, H, HD, HD), 1.0)
    d['state_mlstm_C'] = nrm((DEPTH, DEC_BATCH, H, HD, HD), 0.5)
    d['state_mlstm_n'] = nrm((DEPTH, DEC_BATCH, H, HD), 0.5)
    d['state_mlstm_m'] = 2.0 + nrm((DEPTH, DEC_BATCH, H), 1.0)
    d['state_hgrn_S'] = nrm((DEPTH, DEC_BATCH, H, HD, HD), 0.5)
    d['ffn1_pre_g'] = gain((DEPTH, D_MODEL))
    d['ffn1_w_gate'] = nrm((DEPTH, D_MODEL, D_FF), D_MODEL ** -0.5)
    d['ffn1_w_up'] = nrm((DEPTH, D_MODEL, D_FF), D_MODEL ** -0.5)
    d['ffn1_w_down'] = nrm((DEPTH, D_FF, D_MODEL), D_FF ** -0.5)
    d['ffn1_post_g'] = gain((DEPTH, D_MODEL))
    d['mix_pre_g'] = gain((DEPTH, D_MODEL))
    d['w_in'] = nrm((DEPTH, D_MODEL, N_IN), D_MODEL ** -0.5)
    d['delta_conv_w'] = nrm((DEPTH, CONV_W, 3 * G), CONV_W ** -0.5)
    d['delta_a_log'] = jnp.log(jax.random.uniform(next(ks), (DEPTH, H), F32, 1.0, 16.0))
    dt = jnp.exp(jax.random.uniform(next(ks), (DEPTH, H), F32, math.log(1e-3), math.log(1e-1)))
    d['delta_dt_bias'] = dt + jnp.log(-jnp.expm1(-dt))
    d['delta_norm_g'] = gain((DEPTH, G))
    d['ret_norm_g'] = gain((DEPTH, G))
    d['mlstm_i_bias'] = nrm((DEPTH, H), 0.1)
    d['mlstm_f_bias'] = jnp.linspace(3.0, 6.0, H, dtype=F32)[None, :] + nrm((DEPTH, H), 0.1)
    d['mlstm_norm_g'] = gain((DEPTH, G))
    d['hgrn_lb_logits'] = nrm((DEPTH, G), 0.1)
    d['hgrn_norm_g'] = gain((DEPTH, G))
    d['w_out'] = nrm((DEPTH, D_MIX, D_MODEL), D_MIX ** -0.5)
    d['mix_post_g'] = gain((DEPTH, D_MODEL))
    d['ffn2_pre_g'] = gain((DEPTH, D_MODEL))
    d['ffn2_w_gate'] = nrm((DEPTH, D_MODEL, D_FF), D_MODEL ** -0.5)
    d['ffn2_w_up'] = nrm((DEPTH, D_MODEL, D_FF), D_MODEL ** -0.5)
    d['ffn2_w_down'] = nrm((DEPTH, D_FF, D_MODEL), D_FF ** -0.5)
    d['ffn2_post_g'] = gain((DEPTH, D_MODEL))
    return d


def reference(x_prompt, x_sample, state_delta_S, state_delta_conv, state_ret_S, state_mlstm_C,
              state_mlstm_n, state_mlstm_m, state_hgrn_S,
              ffn1_pre_g, ffn1_w_gate, ffn1_w_up, ffn1_w_down, ffn1_post_g,
              mix_pre_g, w_in, delta_conv_w, delta_a_log, delta_dt_bias, delta_norm_g,
              ret_norm_g, mlstm_i_bias, mlstm_f_bias, mlstm_norm_g, hgrn_lb_logits, hgrn_norm_g,
              w_out, mix_post_g,
              ffn2_pre_g, ffn2_w_gate, ffn2_w_up, ffn2_w_down, ffn2_post_g):
    prm = dict(ffn1_pre_g=ffn1_pre_g, ffn1_w_gate=ffn1_w_gate, ffn1_w_up=ffn1_w_up,
               ffn1_w_down=ffn1_w_down, ffn1_post_g=ffn1_post_g,
               mix_pre_g=mix_pre_g, w_in=w_in, delta_conv_w=delta_conv_w, delta_a_log=delta_a_log,
               delta_dt_bias=delta_dt_bias, delta_norm_g=delta_norm_g, ret_norm_g=ret_norm_g,
               mlstm_i_bias=mlstm_i_bias, mlstm_f_bias=mlstm_f_bias, mlstm_norm_g=mlstm_norm_g,
               hgrn_lb_logits=hgrn_lb_logits, hgrn_norm_g=hgrn_norm_g, w_out=w_out, mix_post_g=mix_post_g,
               ffn2_pre_g=ffn2_pre_g, ffn2_w_gate=ffn2_w_gate, ffn2_w_up=ffn2_w_up,
               ffn2_w_down=ffn2_w_down, ffn2_post_g=ffn2_post_g)
    y_prompt, p_states = run_group(x_prompt, 0, zero_states(x_prompt.shape[0]), prm)
    s_in = (state_delta_S, state_delta_conv, state_ret_S, state_mlstm_C, state_mlstm_n, state_mlstm_m, state_hgrn_S)
    y_sample, s_states = run_group(x_sample, PAST_LEN, s_in, prm)
    p_delta_S, p_delta_conv, p_ret_S, p_mlstm_C, p_mlstm_n, p_mlstm_m, p_hgrn_S = p_states
    s_delta_S, s_delta_conv, s_ret_S, s_mlstm_C, s_mlstm_n, s_mlstm_m, s_hgrn_S = s_states
    return (y_prompt, y_sample,
            p_delta_S, p_delta_conv, p_ret_S, p_mlstm_C, p_mlstm_n, p_mlstm_m, p_hgrn_S,
            s_delta_S, s_delta_conv, s_ret_S, s_mlstm_C, s_mlstm_n, s_mlstm_m, s_hgrn_S)
```

```python
import functools
import math

import numpy as np
import jax
import jax.numpy as jnp
from jax import lax
from jax.experimental import pallas as pl
from jax.experimental.pallas import tpu as pltpu

F32 = jnp.float32
BF16 = jnp.bfloat16

N_MIXERS = 4
HEAD_DIM = 64
N_HEADS = 4
GROUP_W = N_HEADS * HEAD_DIM
CONV_W = 4
CHUNK = 64
ROPE_BASE = 10000.0
EPS = 1e-6
NEG_BIG = -1e30
LB_FLOOR = 1e-30
PAST_LEN = 16384

LANES = 128
N_MAIN = 16 * GROUP_W
N_Z = N_MAIN + LANES
VMEM_LIMIT = 56 * 1024 * 1024

NN = (((1,), (0,)), ((), ()))
NT = (((1,), (1,)), ((), ()))
TN = (((0,), (0,)), ((), ()))


def _cparams(*sem):
    return pltpu.CompilerParams(dimension_semantics=sem, vmem_limit_bytes=VMEM_LIMIT)


def _dg(a, b, dims):
    return lax.dot_general(a, b, dims, preferred_element_type=F32)


def _mm1(a, b, dims=NN):
    return _dg(a.astype(BF16), b.astype(BF16), dims)


def _split2(x):
    hi = x.astype(BF16)
    lo = (x - hi.astype(F32)).astype(BF16)
    return hi, lo


def _split3(x):
    hi = x.astype(BF16)
    r = x - hi.astype(F32)
    mid = r.astype(BF16)
    lo = (r - mid.astype(F32)).astype(BF16)
    return hi, mid, lo


def _mm3(a, b, dims=NN):
    ah, al = _split2(a)
    bh, bl = _split2(b)
    return _dg(ah, bh, dims) + (_dg(ah, bl, dims) + _dg(al, bh, dims))


def _mm01_l(m01, x, dims=NN):
    hi, mid, lo = _split3(x)
    return _dg(m01, hi, dims) + (_dg(m01, mid, dims) + _dg(m01, lo, dims))


def _mm01_r(x, m01, dims=NN):
    hi, mid, lo = _split3(x)
    return _dg(hi, m01, dims) + (_dg(mid, m01, dims) + _dg(lo, m01, dims))


def _softplus(x):
    return jnp.maximum(x, 0.0) + jnp.log1p(jnp.exp(-jnp.abs(x)))


def _sigmoid(x):
    return jax.nn.sigmoid(x)


def _silu(x):
    return x * jax.nn.sigmoid(x)


def _ffn_body(x_ref, pg_ref, wg_ref, wu_ref, wd_ref, qg_ref, o_ref, h_sc, acc_sc):
    j = pl.program_id(1)

    @pl.when(j == 0)
    def _():
        x = x_ref[...]
        ms = jnp.mean(x * x, axis=-1, keepdims=True)
        h_sc[...] = (x * lax.rsqrt(ms + EPS) * pg_ref[...]).astype(BF16)
        acc_sc[...] = jnp.zeros_like(acc_sc)

    h = h_sc[...]
    a = jnp.dot(h, wg_ref[...], preferred_element_type=F32)
    u = jnp.dot(h, wu_ref[...], preferred_element_type=F32)
    g = (_silu(a) * u).astype(BF16)
    acc_sc[...] += jnp.dot(g, wd_ref[...], preferred_element_type=F32)

    @pl.when(j == pl.num_programs(1) - 1)
    def _():
        f = acc_sc[...]
        ms = jnp.mean(f * f, axis=-1, keepdims=True)
        o_ref[...] = x_ref[...] + 0.5 * (f * lax.rsqrt(ms + EPS) * qg_ref[...])


def _ffn(x, pre_g, wg, wu, wd, post_g, tm, tf):
    t, d = x.shape
    ff = wg.shape[1]
    return pl.pallas_call(
        _ffn_body,
        grid=(t // tm, ff // tf),
        in_specs=[
            pl.BlockSpec((tm, d), lambda i, j: (i, 0)),
            pl.BlockSpec((1, d), lambda i, j: (0, 0)),
            pl.BlockSpec((d, tf), lambda i, j: (0, j)),
            pl.BlockSpec((d, tf), lambda i, j: (0, j)),
            pl.BlockSpec((tf, d), lambda i, j: (j, 0)),
            pl.BlockSpec((1, d), lambda i, j: (0, 0)),
        ],
        out_specs=pl.BlockSpec((tm, d), lambda i, j: (i, 0)),
        out_shape=jax.ShapeDtypeStruct((t, d), F32),
        scratch_shapes=[pltpu.VMEM((tm, d), BF16), pltpu.VMEM((tm, d), F32)],
        compiler_params=_cparams("parallel", "arbitrary"),
    )(x, pre_g.reshape(1, d), wg, wu, wd, post_g.reshape(1, d))


def _proj_in_body(x_ref, g_ref, w_ref, z_ref, h_sc):
    @pl.when(pl.program_id(1) == 0)
    def _():
        x = x_ref[...]
        ms = jnp.mean(x * x, axis=-1, keepdims=True)
        h_sc[...] = (x * lax.rsqrt(ms + EPS) * g_ref[...]).astype(BF16)

    z_ref[...] = jnp.dot(h_sc[...], w_ref[...], preferred_element_type=F32)


def _proj_in(x, g, w, tm, tn):
    t, d = x.shape
    nz = w.shape[1]
    return pl.pallas_call(
        _proj_in_body,
        grid=(t // tm, nz // tn),
        in_specs=[
            pl.BlockSpec((tm, d), lambda i, j: (i, 0)),
            pl.BlockSpec((1, d), lambda i, j: (0, 0)),
            pl.BlockSpec((d, tn), lambda i, j: (0, j)),
        ],
        out_specs=pl.BlockSpec((tm, tn), lambda i, j: (i, j)),
        out_shape=jax.ShapeDtypeStruct((t, nz), F32),
        scratch_shapes=[pltpu.VMEM((tm, d), BF16)],
        compiler_params=_cparams("parallel", "arbitrary"),
    )(x, g.reshape(1, d), w)


def _proj_out_body(o_ref, w_ref, g_ref, x_ref, y_ref):
    mix = jnp.dot(o_ref[...].astype(BF16), w_ref[...], preferred_element_type=F32)
    ms = jnp.mean(mix * mix, axis=-1, keepdims=True)
    y_ref[...] = x_ref[...] + mix * lax.rsqrt(ms + EPS) * g_ref[...]


def _proj_out(o, w, g, x, tm):
    t, d = x.shape
    e = o.shape[1]
    return pl.pallas_call(
        _proj_out_body,
        grid=(t // tm,),
        in_specs=[
            pl.BlockSpec((tm, e), lambda i: (i, 0)),
            pl.BlockSpec((e, d), lambda i: (0, 0)),
            pl.BlockSpec((1, d), lambda i: (0, 0)),
            pl.BlockSpec((tm, d), lambda i: (i, 0)),
        ],
        out_specs=pl.BlockSpec((tm, d), lambda i: (i, 0)),
        out_shape=jax.ShapeDtypeStruct((t, d), F32),
        compiler_params=_cparams("parallel"),
    )(o, w, g.reshape(1, d), x)


def _mixer_body(layer, depth, C, pos0,
                z_ref, cs_ref, cw_ref, rows_ref, ng_ref, lbl_ref,
                dS0, dcv0, rS0, mC0, mn0, mm0, hS0,
                o_ref, dS, dcv, rS, mC, mn, mm, hS,
                cbuf, osc):
    n = pl.program_id(1)
    last = pl.num_programs(1) - 1
    D = HEAD_DIM
    G = GROUP_W

    @pl.when(n == 0)
    def _():
        dS[...] = dS0[...]
        rS[...] = rS0[...]
        mC[...] = mC0[...]
        mn[...] = mn0[...]
        mm[...] = mm0[...]
        hS[...] = hS0[...]
        cbuf[5:8, :] = dcv0[0]

    ri = lax.broadcasted_iota(jnp.int32, (C, C), 0)
    ci = lax.broadcasted_iota(jnp.int32, (C, C), 1)
    causal = ci <= ri
    strict = ci < ri
    eye_c = ci == ri
    tri01 = causal.astype(BF16)
    rcol = lax.broadcasted_iota(jnp.int32, (C, 1), 0)
    levels = [1 << k for k in range(int(math.log2(C)))]

    def level_mask(p):
        k = int(math.log2(2 * p))
        same = lax.shift_right_logical(ri, k) == lax.shift_right_logical(ci, k)
        return same & ((ri & (2 * p - 1)) >= p) & ((ci & (2 * p - 1)) < p)

    ei = lax.broadcasted_iota(jnp.int32, (LANES, LANES), 0)
    ej = lax.broadcasted_iota(jnp.int32, (LANES, LANES), 1)
    eye128 = (ei == ej).astype(BF16)
    gi = lax.broadcasted_iota(jnp.int32, (G, G), 0)
    gj = lax.broadcasted_iota(jnp.int32, (G, G), 1)
    seg01 = (lax.shift_right_logical(gi, 6) == lax.shift_right_logical(gj, 6)).astype(BF16)
    di = lax.broadcasted_iota(jnp.int32, (D, D), 0)
    dj = lax.broadcasted_iota(jnp.int32, (D, D), 1)
    eye_d = di == dj

    def seg_sum(x):
        return _mm01_r(x, seg01)

    zg = z_ref[:, N_MAIN:N_Z]
    rows = rows_ref[...]
    xg = zg + rows[0:1, :]
    lane = lax.broadcasted_iota(jnp.int32, (C, LANES), 1)
    la = -jnp.exp(rows[1:2, :]) * _softplus(xg)
    gates = jnp.where(lane < 4, la,
                      jnp.where(lane < 8, _sigmoid(zg),
                                jnp.where(lane < 12, xg,
                                          jnp.where(lane < 16, -_softplus(-xg), 0.0))))
    gates_c = _mm01_l(tri01, gates)
    gates_t = _mm01_l(eye128, gates, NT)
    gates_ct = _mm01_l(eye128, gates_c, NT)

    def hs(h):
        return slice(h * D, (h + 1) * D)

    aqkv = z_ref[:, 0:3 * G]
    cbuf[8:8 + C, :] = aqkv
    cw = cw_ref[...]
    conv = cbuf[5:5 + C, :] * cw[0:1, :]
    conv = conv + cbuf[6:6 + C, :] * cw[1:2, :]
    conv = conv + cbuf[7:7 + C, :] * cw[2:3, :]
    conv = conv + aqkv * cw[3:4, :]
    tail = cbuf[C + 5:C + 8, :]
    cbuf[5:8, :] = tail
    act = _silu(conv)
    aq = act[:, 0:G]
    ak = act[:, G:2 * G]
    av = act[:, 2 * G:3 * G]
    aq = aq * lax.rsqrt(seg_sum(aq * aq) + EPS) * (HEAD_DIM ** -0.5)
    ak = ak * lax.rsqrt(seg_sum(ak * ak) + EPS)

    for h in range(N_HEADS):
        q, k, v = aq[:, hs(h)], ak[:, hs(h)], av[:, hs(h)]
        g_col = gates_c[:, h:h + 1]
        g_row = gates_ct[h:h + 1, :]
        beta = gates[:, 4 + h:5 + h]
        decay = jnp.where(causal, jnp.exp(jnp.where(causal, g_col - g_row, 0.0)), 0.0)
        kk = _mm1(k, k, NT)
        a_mat = jnp.where(strict, beta * decay * kk, 0.0)
        x_inv = jnp.where(eye_c, 1.0, 0.0) - jnp.where(level_mask(1), a_mat, 0.0)
        for p in levels[1:]:
            a_off = jnp.where(level_mask(p), a_mat, 0.0)
            x_inv = x_inv - _mm3(_mm3(x_inv, a_off), x_inv)
        eg = jnp.exp(g_col)
        sol_v = _mm3(x_inv, beta * v)
        sol_k = _mm3(x_inv, (beta * eg) * k)
        s_old = dS[0, h]
        u = sol_v - _mm1(sol_k, s_old)
        att = _mm1(q, k, NT) * decay
        o = eg * _mm1(q, s_old) + _mm1(att, u)
        g_last = gates_c[C - 1:C, h:h + 1]
        dS[0, h] = jnp.exp(g_last) * s_old + _mm1(k * jnp.exp(g_last - g_col), u, TN)
        osc[:, hs(h)] = o

    cos_t = cs_ref[:, 0:G]
    sin_t = cs_ref[:, G:2 * G]
    lane_g = lax.broadcasted_iota(jnp.int32, (C, G), 1)
    first_half = (lane_g & (D - 1)) < (D // 2)

    def rope(x):
        swapped = jnp.where(first_half, pltpu.roll(x, G - D // 2, 1), pltpu.roll(x, D // 2, 1))
        return x * cos_t + swapped * sin_t

    bq = rope(z_ref[:, 4 * G:5 * G])
    bk = rope(z_ref[:, 5 * G:6 * G]) * (HEAD_DIM ** -0.5)
    bv = z_ref[:, 6 * G:7 * G]
    tdiff = (ri - ci).astype(F32)
    tcol = rcol.astype(F32)
    for h in range(N_HEADS):
        lg = float(np.log1p(-np.exp2(np.float32(-5.0 - h))))
        q, k, v = bq[:, hs(h)], bk[:, hs(h)], bv[:, hs(h)]
        decay = jnp.where(causal, jnp.exp(jnp.where(causal, tdiff * lg, 0.0)), 0.0)
        q_dec = jnp.exp((tcol + 1.0) * lg)
        k_dec = jnp.exp((C - 1.0 - tcol) * lg)
        chunk_dec = math.exp(C * lg)
        s_old = rS[0, h]
        att = _mm1(q, k, NT) * decay
        o = _mm1(att, v) + q_dec * _mm1(q, s_old)
        rS[0, h] = chunk_dec * s_old + _mm1(k * k_dec, v, TN)
        osc[:, G + h * D:G + (h + 1) * D] = o

    cq = z_ref[:, 8 * G:9 * G]
    ck = z_ref[:, 9 * G:10 * G] * (HEAD_DIM ** -0.5)
    cv = z_ref[:, 10 * G:11 * G]
    for h in range(N_HEADS):
        q, k, v = cq[:, hs(h)], ck[:, hs(h)], cv[:, hs(h)]
        b_col = gates_c[:, 12 + h:13 + h]
        b_row = gates_ct[12 + h:13 + h, :]
        i_col = gates[:, 8 + h:9 + h]
        i_row = gates_t[8 + h:9 + h, :]
        m_prev = mm[0, :, h:h + 1]
        c_old = mC[0, h]
        n_old = mn[0, h:h + 1, :]
        log_d = jnp.where(causal, b_col - b_row + i_row, NEG_BIG)
        log_inter = b_col + m_prev
        m_t = jnp.maximum(log_inter, jnp.max(log_d, axis=-1, keepdims=True))
        w = jnp.where(causal, jnp.exp(jnp.where(causal, log_d - m_t, 0.0)), 0.0)
        g_in = jnp.exp(log_inter - m_t)
        s = _mm1(q, k, NT) * w
        num = g_in * _mm1(q, c_old) + _mm1(s, v)
        den = g_in * jnp.sum(q * n_old, axis=-1, keepdims=True) + jnp.sum(s, axis=-1, keepdims=True)
        hout = num / jnp.maximum(jnp.abs(den), jnp.exp(-m_t))
        m_new = m_t[C - 1:C, :]
        b_last = b_col[C - 1:C, :]
        dec = jnp.exp(b_last + m_prev - m_new)
        w_s = jnp.exp(b_last - b_col + i_col - m_new)
        kw = k * w_s
        mC[0, h] = dec * c_old + _mm1(kw, v, TN)
        mn[0, h:h + 1, :] = dec * n_old + jnp.sum(kw, axis=0, keepdims=True)
        mm[0, :, h:h + 1] = m_new
        osc[:, 2 * G + h * D:2 * G + (h + 1) * D] = hout

    lbl = lbl_ref[...]
    sm = jnp.exp(lbl - jnp.max(lbl, axis=0, keepdims=True))
    sm = sm / jnp.sum(sm, axis=0, keepdims=True)
    csum = sm[0:1, :]
    for i in range(1, layer + 1):
        csum = csum + sm[i:i + 1, :]
    lb = csum - sm[0:1, :]
    dq = _silu(z_ref[:, 12 * G:13 * G])
    df = z_ref[:, 13 * G:14 * G]
    dv = z_ref[:, 14 * G:15 * G]
    t1 = jnp.log(jnp.maximum(lb, LB_FLOOR))
    t2 = jnp.log1p(-lb) + (-_softplus(-df))
    lfd = jnp.maximum(t1, t2) + jnp.log1p(jnp.exp(-jnp.abs(t1 - t2)))
    kd = (1.0 - lb) * _sigmoid(-df)
    bcum = _mm01_l(tri01, lfd)
    atts = [jnp.where(eye_c, jnp.sum(dq[:, hs(h)] * kd[:, hs(h)], axis=-1, keepdims=True), 0.0)
            for h in range(N_HEADS)]
    for p in levels:
        k2 = int(math.log2(2 * p))
        ref_idx = lax.shift_left(lax.shift_right_logical(ri, k2), k2) + (p - 1)
        sel01 = (ci == ref_idx).astype(BF16)
        bref = _mm01_l(sel01, bcum)
        upper = (rcol & (2 * p - 1)) >= p
        qs = jnp.where(upper, dq * jnp.exp(jnp.where(upper, bcum - bref, 0.0)), 0.0)
        ks = jnp.where(upper, 0.0, kd * jnp.exp(jnp.where(upper, 0.0, bref - bcum)))
        msk = level_mask(p)
        for h in range(N_HEADS):
            atts[h] = atts[h] + jnp.where(msk, _mm3(qs[:, hs(h)], ks[:, hs(h)], NT), 0.0)
    eb = jnp.exp(bcum)
    b_last = bcum[C - 1:C, :]
    eb_last = jnp.exp(b_last)
    k_tail = kd * jnp.exp(b_last - bcum)
    for h in range(N_HEADS):
        s_old = hS[0, h]
        o = _mm1(dq[:, hs(h)] * eb[:, hs(h)], s_old) + _mm1(atts[h], dv[:, hs(h)])
        scale_col = jnp.sum(jnp.where(eye_d, eb_last[:, hs(h)], 0.0), axis=-1, keepdims=True)
        hS[0, h] = scale_col * s_old + _mm1(k_tail[:, hs(h)], dv[:, hs(h)], TN)
        osc[:, 3 * G + h * D:3 * G + (h + 1) * D] = o

    ng = ng_ref[...]
    inv_d = 1.0 / HEAD_DIM

    def head_rms(o, g_row):
        return o * lax.rsqrt(seg_sum(o * o) * inv_d + EPS) * g_row

    ao = head_rms(osc[:, 0:G], ng[0:1, :]) * _silu(z_ref[:, 3 * G:4 * G])
    bo = osc[:, G:2 * G]
    cen = bo - seg_sum(bo) * inv_d
    bo = cen * lax.rsqrt(seg_sum(cen * cen) * inv_d + EPS) * ng[1:2, :] * _silu(z_ref[:, 7 * G:8 * G])
    co = head_rms(osc[:, 2 * G:3 * G], ng[2:3, :]) * _sigmoid(z_ref[:, 11 * G:12 * G])
    do = head_rms(osc[:, 3 * G:4 * G], ng[3:4, :]) * _silu(z_ref[:, 15 * G:16 * G])
    o_ref[:, 0:G] = ao
    o_ref[:, G:2 * G] = bo
    o_ref[:, 2 * G:3 * G] = co
    o_ref[:, 3 * G:4 * G] = do

    @pl.when(n == last)
    def _():
        dcv[0] = cbuf[5:8, :]


def _mixer(z, row0, B, L, C, layer, depth, pos0, cossin, conv_w, rows, norm_g, lb_logits, states):
    N = L // C
    blk0 = row0 // C
    H, D = N_HEADS, HEAD_DIM
    dS0, dcv0, rS0, mC0, mn0, mm0, hS0 = states
    mm0p = jnp.pad(mm0, ((0, 0), (0, LANES - H))).reshape(B, 1, LANES)

    mat_spec = pl.BlockSpec((1, H, D, D), lambda b, n: (b, 0, 0, 0))
    cv_spec = pl.BlockSpec((1, CONV_W - 1, 3 * GROUP_W), lambda b, n: (b, 0, 0))
    n_spec = pl.BlockSpec((1, H, D), lambda b, n: (b, 0, 0))
    m_spec = pl.BlockSpec((1, 1, LANES), lambda b, n: (b, 0, 0))
    state_specs = [mat_spec, cv_spec, mat_spec, mat_spec, n_spec, m_spec, mat_spec]

    def full(a):
        return pl.BlockSpec(a.shape, lambda b, n: (0,) * a.ndim)

    out_shapes = [
        jax.ShapeDtypeStruct((B * L, N_MIXERS * GROUP_W), F32),
        jax.ShapeDtypeStruct((B, H, D, D), F32),
        jax.ShapeDtypeStruct((B, CONV_W - 1, 3 * GROUP_W), F32),
        jax.ShapeDtypeStruct((B, H, D, D), F32),
        jax.ShapeDtypeStruct((B, H, D, D), F32),
        jax.ShapeDtypeStruct((B, H, D), F32),
        jax.ShapeDtypeStruct((B, 1, LANES), F32),
        jax.ShapeDtypeStruct((B, H, D, D), F32),
    ]
    outs = pl.pallas_call(
        functools.partial(_mixer_body, layer, depth, C, pos0),
        grid=(B, N),
        in_specs=[
            pl.BlockSpec((C, N_Z), lambda b, n: (blk0 + b * N + n, 0)),
            pl.BlockSpec((C, 2 * GROUP_W), lambda b, n: (n, 0)),
            full(conv_w), full(rows), full(norm_g), full(lb_logits),
        ] + state_specs,
        out_specs=[pl.BlockSpec((C, N_MIXERS * GROUP_W), lambda b, n: (b * N + n, 0))] + state_specs,
        out_shape=out_shapes,
        scratch_shapes=[pltpu.VMEM((C + 8, 3 * GROUP_W), F32),
                        pltpu.VMEM((C, N_MIXERS * GROUP_W), F32)],
        compiler_params=_cparams("arbitrary", "arbitrary"),
    )(z, cossin, conv_w, rows, norm_g, lb_logits, dS0, dcv0, rS0, mC0, mn0, mm0p, hS0)
    o, dS, dcv, rS, mC, mn, mmp, hS = outs
    return o, (dS, dcv, rS, mC, mn, mmp[:, 0, :H], hS)


def _rope_table(pos0, L):
    half = HEAD_DIM // 2
    pos = pos0 + jnp.arange(L, dtype=F32)
    inv = ROPE_BASE ** (-jnp.arange(half, dtype=F32) / half)
    ang = pos[:, None] * inv[None, :]
    cos, sin = jnp.cos(ang), jnp.sin(ang)
    cos_t = jnp.tile(jnp.concatenate([cos, cos], axis=-1), (1, N_HEADS))
    sin_t = jnp.tile(jnp.concatenate([-sin, sin], axis=-1), (1, N_HEADS))
    return jnp.concatenate([cos_t, sin_t], axis=-1)


def _permute_w_in(w):
    g, h = GROUP_W, N_HEADS
    a0, b0 = 0, 4 * g + 2 * h
    c0 = b0 + 4 * g
    d0 = c0 + 4 * g + 2 * h
    main = jnp.concatenate([w[:, a0:a0 + 4 * g], w[:, b0:b0 + 4 * g],
                            w[:, c0:c0 + 4 * g], w[:, d0:d0 + 4 * g]], axis=1)
    gate = jnp.concatenate([w[:, a0 + 4 * g:a0 + 4 * g + 2 * h], w[:, c0 + 4 * g:c0 + 4 * g + 2 * h]], axis=1)
    gate = jnp.pad(gate, ((0, 0), (0, LANES - 4 * h)))
    return jnp.concatenate([main, gate], axis=1)


def _gate_rows(a_log, dt_bias, i_bias, f_bias):
    h = N_HEADS
    z4 = jnp.zeros((h,), F32)
    bias = jnp.concatenate([dt_bias, z4, i_bias, f_bias, jnp.zeros((LANES - 4 * h,), F32)])
    alog = jnp.concatenate([a_log, jnp.zeros((LANES - h,), F32)])
    return jnp.concatenate([bias[None], alog[None], jnp.zeros((6, LANES), F32)], axis=0)


def _pick(total, prefs):
    for p in prefs:
        if total % p == 0:
            return p
    return total


def kernel(x_prompt, x_sample, state_delta_S, state_delta_conv, state_ret_S, state_mlstm_C, state_mlstm_n, state_mlstm_m, state_hgrn_S, ffn1_pre_g, ffn1_w_gate, ffn1_w_up, ffn1_w_down, ffn1_post_g, mix_pre_g, w_in, delta_conv_w, delta_a_log, delta_dt_bias, delta_norm_g, ret_norm_g, mlstm_i_bias, mlstm_f_bias, mlstm_norm_g, hgrn_lb_logits, hgrn_norm_g, w_out, mix_post_g, ffn2_pre_g, ffn2_w_gate, ffn2_w_up, ffn2_w_down, ffn2_post_g):
    bp, lp, d = x_prompt.shape
    bs, ls, _ = x_sample.shape
    depth = w_in.shape[0]
    tp, ts = bp * lp, bs * ls
    t = tp + ts
    cp, cs = math.gcd(lp, CHUNK), math.gcd(ls, CHUNK)
    assert tp % cs == 0 and lp % cp == 0 and ls % cs == 0
    tm = _pick(t, (1024, 512, 256, 128, 64, 32, 16, 8))
    ff = ffn1_w_gate.shape[2]
    tf = _pick(ff, (256, 128))
    tn = _pick(N_Z, (1408, 384, 128))

    x = jnp.concatenate([x_prompt.reshape(tp, d), x_sample.reshape(ts, d)], axis=0)
    rope_p = _rope_table(0.0, lp)
    rope_s = _rope_table(float(PAST_LEN), ls)
    hd = (N_HEADS, HEAD_DIM, HEAD_DIM)
    zero_states = (jnp.zeros((bp,) + hd, F32), jnp.zeros((bp, CONV_W - 1, 3 * GROUP_W), F32),
                   jnp.zeros((bp,) + hd, F32), jnp.zeros((bp,) + hd, F32),
                   jnp.zeros((bp, N_HEADS, HEAD_DIM), F32), jnp.zeros((bp, N_HEADS), F32),
                   jnp.zeros((bp,) + hd, F32))
    s_in = (state_delta_S, state_delta_conv, state_ret_S, state_mlstm_C, state_mlstm_n,
            state_mlstm_m, state_hgrn_S)

    p_new, s_new = [], []
    for l in range(depth):
        x = _ffn(x, ffn1_pre_g[l], ffn1_w_gate[l].astype(BF16), ffn1_w_up[l].astype(BF16),
                 ffn1_w_down[l].astype(BF16), ffn1_post_g[l], tm, tf)
        z = _proj_in(x, mix_pre_g[l], _permute_w_in(w_in[l]).astype(BF16), tm, tn)
        rows = _gate_rows(delta_a_log[l], delta_dt_bias[l], mlstm_i_bias[l], mlstm_f_bias[l])
        norm_g = jnp.stack([delta_norm_g[l], ret_norm_g[l], mlstm_norm_g[l], hgrn_norm_g[l]], axis=0)
        o_p, st_p = _mixer(z, 0, bp, lp, cp, l, depth, 0.0, rope_p, delta_conv_w[l], rows, norm_g,
                           hgrn_lb_logits, zero_states)
        o_s, st_s = _mixer(z, tp, bs, ls, cs, l, depth, float(PAST_LEN), rope_s, delta_conv_w[l], rows,
                           norm_g, hgrn_lb_logits, tuple(s[l] for s in s_in))
        p_new.append(st_p)
        s_new.append(st_s)
        o = jnp.concatenate([o_p, o_s], axis=0)
        x = _proj_out(o, w_out[l].astype(BF16), mix_post_g[l], x, tm)
        x = _ffn(x, ffn2_pre_g[l], ffn2_w_gate[l].astype(BF16), ffn2_w_up[l].astype(BF16),
                 ffn2_w_down[l].astype(BF16), ffn2_post_g[l], tm, tf)

    y_prompt = x[:tp].reshape(bp, lp, d)
    y_sample = x[tp:].reshape(bs, ls, d)
    p_states = tuple(jnp.stack([p_new[l][i] for l in range(depth)], axis=0) for i in range(7))
    s_states = tuple(jnp.stack([s_new[l][i] for l in range(depth)], axis=0) for i in range(7))
    return (y_prompt, y_sample) + p_states + s_states
```
